```python
import math, functools
import jax, jax.numpy as jnp
from jax import lax
import numpy as np

D_MODEL = 1024
BATCH = 8
SEQ = 2048
DEPTH = 1
DEC_BATCH = 128
DEC_SEQ = 1
PAST_LEN = 8192
PAGE_SIZE = 128

MIX_WIDTH = D_MODEL
ATTN_WIDTH = MIX_WIDTH // 2
CONV_WIDTH = MIX_WIDTH - ATTN_WIDTH
ATTN_HEAD_DIM = 64
V_HEAD_DIM = 2 * ATTN_HEAD_DIM
N_ATTN_HEADS = ATTN_WIDTH // V_HEAD_DIM
QK_WIDTH = N_ATTN_HEADS * 2 * ATTN_HEAD_DIM
IN_WIDTH = 2 * QK_WIDTH + ATTN_WIDTH + 3 * CONV_WIDTH
CONV_K = 3
N_EXPERTS = 32
TOP_K = 4
D_FF = D_MODEL
SWIGLU_LIMIT = 7.0
SWIGLU_ALPHA = 1.702
ROPE_THETA = 10000.0
EPS = 1e-6
Q_BLOCK = 128

kernel_name = "hymba_diffattn_shortconv_moe_step"


def _rmsnorm(x, g):
    xf = x.astype(jnp.float32)
    y = xf * lax.rsqrt(jnp.mean(xf * xf, axis=-1, keepdims=True) + EPS)
    return (y * g.astype(jnp.float32)).astype(x.dtype)


def _ada(c, w, b):
    m = jnp.einsum('bd,de->be', jax.nn.silu(c), w) + b
    return jnp.split(m, 6, axis=-1)


def _modulate(h, shift, scale):
    return h * (1.0 + scale[:, None, :]) + shift[:, None, :]


def _rope(x, pos):
    half = ATTN_HEAD_DIM // 2
    inv = ROPE_THETA ** (-jnp.arange(half, dtype=jnp.float32) / half)
    ang = pos.astype(jnp.float32)[:, None] * inv[None, :]
    cos = jnp.cos(ang)[None, :, None, None, :]
    sin = jnp.sin(ang)[None, :, None, None, :]
    xf = x.astype(jnp.float32)
    x1, x2 = xf[..., :half], xf[..., half:]
    return jnp.concatenate([x1 * cos - x2 * sin, x2 * cos + x1 * sin], axis=-1).astype(x.dtype)


def _project(h, w_in, pos):
    z = jnp.einsum('bsd,de->bse', h, w_in)
    b_, s_ = h.shape[0], h.shape[1]
    i1 = QK_WIDTH
    i2 = 2 * QK_WIDTH
    i3 = i2 + ATTN_WIDTH
    i4 = i3 + CONV_WIDTH
    i5 = i4 + CONV_WIDTH
    q, k, v, b_gate, c_gate, x_conv = jnp.split(z, [i1, i2, i3, i4, i5], axis=-1)
    q = _rope(q.reshape(b_, s_, N_ATTN_HEADS, 2, ATTN_HEAD_DIM), pos)
    k = _rope(k.reshape(b_, s_, N_ATTN_HEADS, 2, ATTN_HEAD_DIM), pos)
    v = v.reshape(b_, s_, N_ATTN_HEADS, V_HEAD_DIM)
    return q, k, v, b_gate, c_gate * x_conv


def _diff_lambda(lq1, lk1, lq2, lk2, lam_init):
    f = jnp.float32
    return (jnp.exp(jnp.sum(lq1.astype(f) * lk1.astype(f)))
            - jnp.exp(jnp.sum(lq2.astype(f) * lk2.astype(f))) + lam_init)


def _diff_attn_prompt(q, k, v, lam):
    b_, s_ = q.shape[0], q.shape[1]
    scale = ATTN_HEAD_DIM ** -0.5
    key_pos = jnp.arange(s_)

    def block(i):
        start = i * Q_BLOCK
        qb = lax.dynamic_slice_in_dim(q, start, Q_BLOCK, axis=1)
        s = jnp.einsum('bqhcd,bkhcd->bhcqk', qb, k).astype(jnp.float32) * scale
        q_pos = start + jnp.arange(Q_BLOCK)
        s = jnp.where(key_pos[None, :] <= q_pos[:, None], s, -jnp.inf)
        p = jax.nn.softmax(s, axis=-1)
        a = p[:, :, 0] - lam * p[:, :, 1]
        return jnp.einsum('bhqk,bkhe->bqhe', a, v.astype(jnp.float32))

    out = lax.map(block, jnp.arange(s_ // Q_BLOCK))
    return jnp.moveaxis(out, 0, 1).reshape(b_, s_, N_ATTN_HEADS, V_HEAD_DIM)


def _diff_attn_sample(q, k, v, lam, cache_k, cache_v, page_table, layer):
    db, ds = q.shape[0], q.shape[1]
    past_len = page_table.shape[1] * cache_k.shape[2]
    scale = ATTN_HEAD_DIM ** -0.5
    k_past = cache_k[layer, page_table].reshape(db, past_len, N_ATTN_HEADS, 2, ATTN_HEAD_DIM)
    v_past = cache_v[layer, page_table].reshape(db, past_len, N_ATTN_HEADS, V_HEAD_DIM)
    s_past = jnp.einsum('bqhcd,bkhcd->bhcqk', q, k_past).astype(jnp.float32)
    s_new = jnp.einsum('bqhcd,bkhcd->bhcqk', q, k).astype(jnp.float32)
    causal = jnp.arange(ds)[None, :] <= jnp.arange(ds)[:, None]
    s_new = jnp.where(causal, s_new, -jnp.inf)
    s = jnp.concatenate([s_past, s_new], axis=-1) * scale
    p = jax.nn.softmax(s, axis=-1)
    a = p[:, :, 0] - lam * p[:, :, 1]
    return (jnp.einsum('bhqk,bkhe->bqhe', a[..., :past_len], v_past.astype(jnp.float32))
            + jnp.einsum('bhqk,bkhe->bqhe', a[..., past_len:], v.astype(jnp.float32)))


def _short_conv(prefix, u, w):
    full = jnp.concatenate([prefix.astype(u.dtype), u], axis=1)
    s_ = u.shape[1]
    y = full[:, 0:s_] * w[0]
    for j in range(1, CONV_K):
        y = y + full[:, j:j + s_] * w[j]
    return y, full[:, -(CONV_K - 1):]


def _moe(h, w_router, b_router, w_gate_up, b_gate_up, w_down, b_down):
    logits = (jnp.einsum('bsd,de->bse', h, w_router) + b_router).astype(jnp.float32)
    top_val, top_idx = lax.top_k(logits, TOP_K)
    top_w = jax.nn.softmax(top_val, axis=-1)
    gates = jnp.einsum('bsk,bske->bse', top_w, jax.nn.one_hot(top_idx, N_EXPERTS, dtype=jnp.float32))
    out = jnp.zeros(h.shape, jnp.float32)
    for e in range(N_EXPERTS):
        gu = jnp.einsum('bsd,df->bsf', h, w_gate_up[e]) + b_gate_up[e]
        g = jnp.minimum(gu[..., :D_FF], SWIGLU_LIMIT)
        u = jnp.clip(gu[..., D_FF:], -SWIGLU_LIMIT, SWIGLU_LIMIT)
        act = (u + 1.0) * g * jax.nn.sigmoid(SWIGLU_ALPHA * g)
        y = jnp.einsum('bsf,fd->bsd', act, w_down[e]) + b_down[e]
        out = out + gates[..., e:e + 1] * y
    return out.astype(h.dtype)


def _layer(x, c, pos, conv_prefix, attend, lam_init, p):
    (w_ada, b_ada, g_pre_mix, g_post_mix, g_pre_ffn, g_post_ffn, w_in, lam_q1, lam_k1, lam_q2,
     lam_k2, g_subln, conv_w, w_out, w_router, b_router, w_gate_up, b_gate_up, w_down, b_down) = p
    b_, s_ = x.shape[0], x.shape[1]
    sh1, sc1, ga1, sh2, sc2, ga2 = _ada(c, w_ada, b_ada)
    h = _modulate(_rmsnorm(x, g_pre_mix), sh1, sc1)
    q, k, v, b_gate, u = _project(h, w_in, pos)
    lam = _diff_lambda(lam_q1, lam_k1, lam_q2, lam_k2, lam_init)
    o = attend(q, k, v, lam)
    o = (_rmsnorm(o, g_subln) * (1.0 - lam_init)).astype(x.dtype).reshape(b_, s_, ATTN_WIDTH)
    y_conv, conv_state = _short_conv(conv_prefix, u, conv_w)
    mixed = jnp.concatenate([o, b_gate * y_conv], axis=-1)
    mix_out = jnp.einsum('bsm,md->bsd', mixed, w_out)
    x = x + ga1[:, None, :] * _rmsnorm(mix_out, g_post_mix)
    h = _modulate(_rmsnorm(x, g_pre_ffn), sh2, sc2)
    ffn = _moe(h, w_router, b_router, w_gate_up, b_gate_up, w_down, b_down)
    x = x + ga2[:, None, :] * _rmsnorm(ffn, g_post_ffn)
    k_rows = k.reshape(b_, s_, N_ATTN_HEADS, 2 * ATTN_HEAD_DIM)
    return x, k_rows, v, conv_state


def setup_inputs(seed: int = 0) -> dict:
    key = jax.random.key(seed)
    ks = jax.random.split(key, 32)
    f = jnp.float32
    n_pages = PAST_LEN // PAGE_SIZE
    n_used = DEC_BATCH * n_pages
    n_phys = n_used + (n_used + 3) // 4
    nrm = lambda k, shape, s: jax.random.normal(k, shape, f) * s
    perm = jax.random.permutation(ks[7], n_phys)[:n_used]
    return {
        "x_prompt": nrm(ks[0], (BATCH, SEQ, D_MODEL), 1.0),
        "x_sample": nrm(ks[1], (DEC_BATCH, DEC_SEQ, D_MODEL), 1.0),
        "c_prompt": nrm(ks[2], (BATCH, D_MODEL), 1.0),
        "c_sample": nrm(ks[3], (DEC_BATCH, D_MODEL), 1.0),
        "cache_k": nrm(ks[4], (DEPTH, n_phys, PAGE_SIZE, N_ATTN_HEADS, 2 * ATTN_HEAD_DIM), 1.0),
        "cache_v": nrm(ks[5], (DEPTH, n_phys, PAGE_SIZE, N_ATTN_HEADS, V_HEAD_DIM), 1.0),
        "state_conv": nrm(ks[6], (DEPTH, DEC_BATCH, CONV_K - 1, CONV_WIDTH), 1.0),
        "page_table": perm.reshape(DEC_BATCH, n_pages).astype(jnp.int32),
        "w_ada": nrm(ks[8], (DEPTH, D_MODEL, 6 * D_MODEL), 0.3 * D_MODEL ** -0.5),
        "b_ada": nrm(ks[9], (DEPTH, 6 * D_MODEL), 0.02),
        "g_pre_mix": 1.0 + nrm(ks[10], (DEPTH, D_MODEL), 0.02),
        "g_post_mix": 1.0 + nrm(ks[11], (DEPTH, D_MODEL), 0.02),
        "g_pre_ffn": 1.0 + nrm(ks[12], (DEPTH, D_MODEL), 0.02),
        "g_post_ffn": 1.0 + nrm(ks[13], (DEPTH, D_MODEL), 0.02),
        "w_in": nrm(ks[14], (DEPTH, D_MODEL, IN_WIDTH), D_MODEL ** -0.5),
        "lam_q1": nrm(ks[15], (DEPTH, ATTN_HEAD_DIM), 0.1),
        "lam_k1": nrm(ks[16], (DEPTH, ATTN_HEAD_DIM), 0.1),
        "lam_q2": nrm(ks[17], (DEPTH, ATTN_HEAD_DIM), 0.1),
        "lam_k2": nrm(ks[18], (DEPTH, ATTN_HEAD_DIM), 0.1),
        "g_subln": 1.0 + nrm(ks[19], (DEPTH, V_HEAD_DIM), 0.02),
        "conv_w": nrm(ks[20], (DEPTH, CONV_K, CONV_WIDTH), CONV_K ** -0.5),
        "w_out": nrm(ks[21], (DEPTH, MIX_WIDTH, D_MODEL), MIX_WIDTH ** -0.5),
        "w_router": nrm(ks[22], (DEPTH, D_MODEL, N_EXPERTS), D_MODEL ** -0.5),
        "b_router": nrm(ks[23], (DEPTH, N_EXPERTS), 0.01),
        "w_gate_up": nrm(ks[24], (DEPTH, N_EXPERTS, D_MODEL, 2 * D_FF), D_MODEL ** -0.5),
        "b_gate_up": nrm(ks[25], (DEPTH, N_EXPERTS, 2 * D_FF), 0.02),
        "w_down": nrm(ks[26], (DEPTH, N_EXPERTS, D_FF, D_MODEL), D_FF ** -0.5),
        "b_down": nrm(ks[27], (DEPTH, N_EXPERTS, D_MODEL), 0.02),
    }


def reference(x_prompt, x_sample, c_prompt, c_sample, cache_k, cache_v, state_conv, page_table,
              w_ada, b_ada, g_pre_mix, g_post_mix, g_pre_ffn, g_post_ffn, w_in, lam_q1, lam_k1,
              lam_q2, lam_k2, g_subln, conv_w, w_out, w_router, b_router, w_gate_up, b_gate_up,
              w_down, b_down):
    past_len = page_table.shape[1] * cache_k.shape[2]
    pos_p = jnp.arange(x_prompt.shape[1], dtype=jnp.int32)
    pos_s = past_len + jnp.arange(x_sample.shape[1], dtype=jnp.int32)
    xp, xs = x_prompt, x_sample
    kp_l, vp_l, cp_l, ks_l, vs_l, cs_l = [], [], [], [], [], []
    for l in range(DEPTH):
        lam_init = 0.8 - 0.6 * math.exp(-0.3 * l)
        p = (w_ada[l], b_ada[l], g_pre_mix[l], g_post_mix[l], g_pre_ffn[l], g_post_ffn[l], w_in[l],
             lam_q1[l], lam_k1[l], lam_q2[l], lam_k2[l], g_subln[l], conv_w[l], w_out[l],
             w_router[l], b_router[l], w_gate_up[l], b_gate_up[l], w_down[l], b_down[l])
        prefix_p = jnp.zeros((xp.shape[0], CONV_K - 1, CONV_WIDTH), xp.dtype)
        xp, k_new, v_new, c_new = _layer(xp, c_prompt, pos_p, prefix_p, _diff_attn_prompt, lam_init, p)
        kp_l.append(k_new)
        vp_l.append(v_new)
        cp_l.append(c_new)
        attend_s = functools.partial(_diff_attn_sample, cache_k=cache_k, cache_v=cache_v,
                                     page_table=page_table, layer=l)
        xs, k_new, v_new, c_new = _layer(xs, c_sample, pos_s, state_conv[l], attend_s, lam_init, p)
        ks_l.append(k_new)
        vs_l.append(v_new)
        cs_l.append(c_new)
    return (xp, xs, jnp.stack(kp_l), jnp.stack(vp_l), jnp.stack(cp_l),
            jnp.stack(ks_l), jnp.stack(vs_l), jnp.stack(cs_l))
```

```python
import functools
import math

import jax
import jax.numpy as jnp
from jax import lax
from jax.experimental import pallas as pl
from jax.experimental.pallas import tpu as pltpu

F32 = jnp.float32
BF16 = jnp.bfloat16

D_MODEL = 1024
N_HEADS = 4
HEAD_DIM = 64
V_DIM = 2 * HEAD_DIM
QK_W = N_HEADS * 2 * HEAD_DIM
ATTN_W = N_HEADS * V_DIM
CONV_W = 512
CONV_K = 3
N_EXPERTS = 32
TOP_K = 4
D_FF = 1024
SWIGLU_LIMIT = 7.0
SWIGLU_ALPHA = 1.702
ROPE_THETA = 10000.0
EPS = 1e-6
LAM_INIT = 0.8 - 0.6 * math.exp(-0.3 * 0)
LANES = 128
SUBLANES = 8

TM = 256
TQ = 256
TKV = 256
PAGES_PER_STEP = 8
TM_E = 256
VMEM_LIMIT = 48 * 1024 * 1024


def _cparams(sem):
    return pltpu.CompilerParams(dimension_semantics=sem, vmem_limit_bytes=VMEM_LIMIT)


def _rms(x):
    return x * lax.rsqrt(jnp.mean(x * x, axis=-1, keepdims=True) + EPS)


def _dot(a, b):
    return jnp.dot(a, b, preferred_element_type=F32)


def _ada_kernel(c_ref, w_ref, b_ref, o_ref):
    c = c_ref[...]
    s = (c * jax.nn.sigmoid(c)).astype(BF16)
    o_ref[...] = _dot(s, w_ref[...].astype(BF16)) + b_ref[...]


def _ada(c_all, w_ada, b_ada):
    n = c_all.shape[0]
    tn = 1024
    return pl.pallas_call(
        _ada_kernel,
        grid=(w_ada.shape[1] // tn,),
        in_specs=[pl.BlockSpec((n, D_MODEL), lambda j: (0, 0)),
                  pl.BlockSpec((D_MODEL, tn), lambda j: (0, j)),
                  pl.BlockSpec((1, tn), lambda j: (0, j))],
        out_specs=pl.BlockSpec((n, tn), lambda j: (0, j)),
        out_shape=jax.ShapeDtypeStruct((n, w_ada.shape[1]), F32),
        compiler_params=_cparams(("arbitrary",)),
        name="ada",
    )(c_all, w_ada, b_ada.reshape(1, -1))


def _rope128(z, cos, sin_signed):
    lane = lax.broadcasted_iota(jnp.int32, (z.shape[0], LANES), 1)
    first_half = (lane % HEAD_DIM) < (HEAD_DIM // 2)
    outs = []
    for i in range(z.shape[1] // LANES):
        xc = z[:, i * LANES:(i + 1) * LANES]
        partner = jnp.where(first_half,
                            pltpu.roll(xc, LANES - HEAD_DIM // 2, 1),
                            pltpu.roll(xc, HEAD_DIM // 2, 1))
        outs.append(xc * cos + partner * sin_signed)
    return jnp.concatenate(outs, axis=1)


def _project(x, g, shift, scale, w_ref, cos, sin_signed):
    h = _rms(x) * g
    hb = (h * (1.0 + scale) + shift).astype(BF16)
    proj = lambda lo, hi: _dot(hb, w_ref[:, lo:hi])
    q = _rope128(proj(0, QK_W), cos, sin_signed) * (HEAD_DIM ** -0.5)
    k = _rope128(proj(QK_W, 2 * QK_W), cos, sin_signed)
    v = proj(2 * QK_W, 2 * QK_W + ATTN_W)
    o = 2 * QK_W + ATTN_W
    b_gate = proj(o, o + CONV_W)
    u = proj(o + CONV_W, o + 2 * CONV_W) * proj(o + 2 * CONV_W, o + 3 * CONV_W)
    return q, k, v, b_gate, u


def _premix_prompt_kernel(x_ref, sh_ref, sc_ref, g_ref, w_ref, cos_ref, sin_ref, cw_ref,
                          q_ref, k_ref, v_ref, kb_ref, vb_ref, mc_ref, cs_ref, ubuf):
    j = pl.program_id(1)
    tm = x_ref.shape[1]
    q, k, v, b_gate, u = _project(x_ref[0], g_ref[...], sh_ref[0], sc_ref[0], w_ref,
                                  cos_ref[...], sin_ref[...])
    q_ref[0] = q.astype(BF16)
    k_ref[0] = k
    v_ref[0] = v
    kb_ref[0] = k.astype(BF16)
    vb_ref[0] = v.astype(BF16)

    @pl.when(j == 0)
    def _():
        ubuf[0:SUBLANES, :] = jnp.zeros((SUBLANES, CONV_W), F32)

    ubuf[SUBLANES:SUBLANES + tm, :] = u
    um1 = ubuf[SUBLANES - 1:SUBLANES - 1 + tm, :]
    um2 = ubuf[SUBLANES - 2:SUBLANES - 2 + tm, :]
    y = um2 * cw_ref[0:1, :] + um1 * cw_ref[1:2, :] + u * cw_ref[2:3, :]
    mc_ref[0] = (b_gate * y).astype(BF16)
    tail = ubuf[tm:tm + SUBLANES, :]
    ubuf[0:SUBLANES, :] = tail
    cs_ref[0] = tail[SUBLANES - (CONV_K - 1):SUBLANES, :]


def _premix_prompt(x, sh, sc, g, w_in_b, cos, sin_signed, conv_w):
    b, s, _ = x.shape
    tok = lambda w, dt: jax.ShapeDtypeStruct((b, s, w), dt)
    tile = lambda w: pl.BlockSpec((1, TM, w), lambda i, j: (i, j, 0))
    per_b = pl.BlockSpec((1, 1, D_MODEL), lambda i, j: (i, 0, 0))
    const = lambda shape: pl.BlockSpec(shape, lambda i, j: tuple(0 for _ in shape))
    return pl.pallas_call(
        _premix_prompt_kernel,
        grid=(b, s // TM),
        in_specs=[tile(D_MODEL), per_b, per_b, const((1, D_MODEL)), const(w_in_b.shape),
                  pl.BlockSpec((TM, LANES), lambda i, j: (j, 0)),
                  pl.BlockSpec((TM, LANES), lambda i, j: (j, 0)),
                  const((CONV_K, CONV_W))],
        out_specs=[tile(QK_W), tile(QK_W), tile(ATTN_W), tile(QK_W), tile(ATTN_W), tile(CONV_W),
                   pl.BlockSpec((1, CONV_K - 1, CONV_W), lambda i, j: (i, 0, 0))],
        out_shape=[tok(QK_W, BF16), tok(QK_W, F32), tok(ATTN_W, F32), tok(QK_W, BF16),
                   tok(ATTN_W, BF16), tok(CONV_W, BF16),
                   jax.ShapeDtypeStruct((b, CONV_K - 1, CONV_W), F32)],
        scratch_shapes=[pltpu.VMEM((TM + SUBLANES, CONV_W), F32)],
        compiler_params=_cparams(("arbitrary", "arbitrary")),
        name="premix_prompt",
    )(x, sh.reshape(b, 1, -1), sc.reshape(b, 1, -1), g.reshape(1, -1), w_in_b, cos, sin_signed, conv_w)


def _premix_sample_kernel(x_ref, sh_ref, sc_ref, g_ref, w_ref, cos_ref, sin_ref, cw_ref, p0_ref, p1_ref,
                          q_ref, k_ref, v_ref, mc_ref, u_ref):
    q, k, v, b_gate, u = _project(x_ref[...], g_ref[...], sh_ref[...], sc_ref[...], w_ref,
                                  cos_ref[...], sin_ref[...])
    q_ref[...] = q
    k_ref[...] = k
    v_ref[...] = v
    u_ref[...] = u
    y = p0_ref[...] * cw_ref[0:1, :] + p1_ref[...] * cw_ref[1:2, :] + u * cw_ref[2:3, :]
    mc_ref[...] = (b_gate * y).astype(BF16)


def _premix_sample(x, sh, sc, g, w_in_b, cos, sin_signed, conv_w, p0, p1):
    n = x.shape[0]
    full = lambda a: pl.BlockSpec(a.shape, lambda i, nd=a.ndim: (0,) * nd)
    args = (x, sh, sc, g.reshape(1, -1), w_in_b, cos, sin_signed, conv_w, p0, p1)
    outs = [jax.ShapeDtypeStruct((n, QK_W), F32), jax.ShapeDtypeStruct((n, QK_W), F32),
            jax.ShapeDtypeStruct((n, ATTN_W), F32), jax.ShapeDtypeStruct((n, CONV_W), BF16),
            jax.ShapeDtypeStruct((n, CONV_W), F32)]
    return pl.pallas_call(
        _premix_sample_kernel,
        grid=(1,),
        in_specs=[full(a) for a in args],
        out_specs=[pl.BlockSpec(o.shape, lambda i: (0, 0)) for o in outs],
        out_shape=outs,
        compiler_params=_cparams(("arbitrary",)),
        name="premix_sample",
    )(*args)


def _lam(lv_ref):
    d1 = jnp.sum(lv_ref[0:1, :] * lv_ref[1:2, :], axis=1, keepdims=True)
    d2 = jnp.sum(lv_ref[2:3, :] * lv_ref[3:4, :], axis=1, keepdims=True)
    return jnp.exp(d1) - jnp.exp(d2) + LAM_INIT


def _subln(o, gsub):
    return _rms(o) * gsub * (1.0 - LAM_INIT)


def _attn_prompt_kernel(lv_ref, gs_ref, q_ref, k_ref, v_ref, o_ref):
    qi = pl.program_id(1)
    tq = q_ref.shape[1]
    lam = _lam(lv_ref)
    lane = lax.broadcasted_iota(jnp.int32, (tq, V_DIM), 1)
    row = lax.broadcasted_iota(jnp.int32, (2 * tq, TKV), 0) % tq
    col = lax.broadcasted_iota(jnp.int32, (2 * tq, TKV), 1)
    causal = col <= row
    for h in range(N_HEADS):
        hs = slice(h * V_DIM, (h + 1) * V_DIM)
        qh = q_ref[0, :, hs]
        zero = jnp.zeros_like(qh)
        qs = jnp.concatenate([jnp.where(lane < HEAD_DIM, qh, zero),
                              jnp.where(lane >= HEAD_DIM, qh, zero)], axis=0)

        def tile(ki, carry, masked):
            m, l, acc = carry
            kt = k_ref[0, pl.ds(pl.multiple_of(ki * TKV, TKV), TKV), hs]
            vt = v_ref[0, pl.ds(pl.multiple_of(ki * TKV, TKV), TKV), hs]
            s = lax.dot_general(qs, kt, (((1,), (1,)), ((), ())), preferred_element_type=F32)
            if masked:
                s = jnp.where(causal, s, -jnp.inf)
            m_new = jnp.maximum(m, jnp.max(s, axis=1, keepdims=True))
            alpha = jnp.exp(m - m_new)
            p = jnp.exp(s - m_new)
            l = alpha * l + jnp.sum(p, axis=1, keepdims=True)
            acc = alpha * acc + _dot(p.astype(BF16), vt)
            return m_new, l, acc

        init = (jnp.full((2 * tq, 1), -jnp.inf, F32), jnp.zeros((2 * tq, 1), F32),
                jnp.zeros((2 * tq, V_DIM), F32))
        carry = lax.fori_loop(0, qi, lambda ki, c: tile(ki, c, False), init)
        m, l, acc = tile(qi, carry, True)
        on = acc / l
        o = on[:tq] - lam * on[tq:]
        o_ref[0, :, hs] = _subln(o, gs_ref[...]).astype(o_ref.dtype)


def _attn_prompt(lam_vecs, g_subln, qb, kb, vb):
    b, s, _ = qb.shape
    const = lambda shape: pl.BlockSpec(shape, lambda i, j: tuple(0 for _ in shape))
    seq = pl.BlockSpec((1, s, QK_W), lambda i, j: (i, 0, 0))
    return pl.pallas_call(
        _attn_prompt_kernel,
        grid=(b, s // TQ),
        in_specs=[const(lam_vecs.shape), const((1, V_DIM)),
                  pl.BlockSpec((1, TQ, QK_W), lambda i, j: (i, j, 0)), seq, seq],
        out_specs=pl.BlockSpec((1, TQ, ATTN_W), lambda i, j: (i, j, 0)),
        out_shape=jax.ShapeDtypeStruct((b, s, ATTN_W), BF16),
        compiler_params=_cparams(("arbitrary", "arbitrary")),
        name="attn_prompt",
    )(lam_vecs, g_subln.reshape(1, -1), qb, kb, vb)


def _attn_decode_kernel(pt_ref, lv_ref, gs_ref, q8_ref, kn_ref, vn_ref, *rest):
    npg = PAGES_PER_STEP
    k_refs, v_refs = rest[:npg], rest[npg:2 * npg]
    o_ref, st = rest[2 * npg], rest[2 * npg + 1]
    c = pl.program_id(1)
    q8 = q8_ref[0]

    @pl.when(c == 0)
    def _():
        st[0:2] = jnp.full((2, SUBLANES, LANES), -jnp.inf, F32)
        st[2:6] = jnp.zeros((4, SUBLANES, LANES), F32)

    r_id = lax.broadcasted_iota(jnp.int32, (LANES, 2 * LANES), 0)
    c_id = lax.broadcasted_iota(jnp.int32, (LANES, 2 * LANES), 1)
    ones2 = jnp.where((r_id // HEAD_DIM) == (c_id // LANES), 1.0, 0.0).astype(BF16)

    def update(state, s0, s1, v3):
        m0, m1, l0, l1, a0, a1 = state
        out = []
        for m, l, a, s in ((m0, l0, a0, s0), (m1, l1, a1, s1)):
            m_new = jnp.maximum(m, jnp.max(s, axis=0))
            alpha = jnp.exp(m - m_new)
            p = jnp.exp(s - m_new[None])
            out.append((m_new, alpha * l + jnp.sum(p, axis=0), alpha * a + jnp.sum(p * v3, axis=0)))
        (m0, l0, a0), (m1, l1, a1) = out
        return m0, m1, l0, l1, a0, a1

    def scores(k2):
        rows = k2.shape[0]
        prod = (k2.reshape(rows // SUBLANES, SUBLANES, LANES) * q8[None]).reshape(rows, LANES)
        s = _dot(prod.astype(BF16), ones2)
        return (s[:, :LANES].reshape(rows // SUBLANES, SUBLANES, LANES),
                s[:, LANES:].reshape(rows // SUBLANES, SUBLANES, LANES))

    state = tuple(st[i] for i in range(6))
    for kr, vr in zip(k_refs, v_refs):
        s0, s1 = scores(kr[...])
        state = update(state, s0, s1, vr[...].reshape(-1, SUBLANES, LANES))
    for i in range(6):
        st[i] = state[i]

    @pl.when(c == pl.num_programs(1) - 1)
    def _():
        s0, s1 = scores(kn_ref[0])
        valid = lax.broadcasted_iota(jnp.int32, (1, SUBLANES, LANES), 1) < N_HEADS
        s0 = jnp.where(valid, s0, -jnp.inf)
        s1 = jnp.where(valid, s1, -jnp.inf)
        m0, m1, l0, l1, a0, a1 = update(state, s0, s1, vn_ref[0][None])
        lam = _lam(lv_ref)

        def merged(m, l, a):
            mh = jnp.maximum(m[:N_HEADS], m[N_HEADS:])
            w_lo, w_hi = jnp.exp(m[:N_HEADS] - mh), jnp.exp(m[N_HEADS:] - mh)
            return (a[:N_HEADS] * w_lo + a[N_HEADS:] * w_hi) / (l[:N_HEADS] * w_lo + l[N_HEADS:] * w_hi)

        o = merged(m0, l0, a0) - lam * merged(m1, l1, a1)
        o_ref[0] = _subln(o, gs_ref[...])


def _attn_decode(page_table, lam_vecs, g_subln, q8, kn8, vn8, ck2, cv2, page_rows):
    nseq, npages = page_table.shape
    npg = PAGES_PER_STEP
    const = lambda shape: pl.BlockSpec(shape, lambda b, c, pt: tuple(0 for _ in shape))
    per_seq = pl.BlockSpec((1, SUBLANES, LANES), lambda b, c, pt: (b, 0, 0))

    def page_spec(i):
        return pl.BlockSpec((page_rows, LANES), lambda b, c, pt: (pt[b * npages + c * npg + i], 0))

    grid_spec = pltpu.PrefetchScalarGridSpec(
        num_scalar_prefetch=1,
        grid=(nseq, npages // npg),
        in_specs=[const(lam_vecs.shape), const((1, V_DIM)), per_seq, per_seq, per_seq]
                 + [page_spec(i) for i in range(npg)] * 2,
        out_specs=pl.BlockSpec((1, N_HEADS, V_DIM), lambda b, c, pt: (b, 0, 0)),
        scratch_shapes=[pltpu.VMEM((6, SUBLANES, LANES), F32)],
    )
    return pl.pallas_call(
        _attn_decode_kernel,
        grid_spec=grid_spec,
        out_shape=jax.ShapeDtypeStruct((nseq, N_HEADS, V_DIM), F32),
        compiler_params=_cparams(("arbitrary", "arbitrary")),
        name="attn_decode",
    )(page_table.reshape(-1), lam_vecs, g_subln.reshape(1, -1), q8, kn8, vn8,
      *([ck2] * npg), *([cv2] * npg))


def _postmix_body(x, o, mc, wo_ref, ga1, g_post, g_pre, sh2, sc2, wr_hi, wr_lo, br):
    mix = _dot(o, wo_ref[0:ATTN_W, :]) + _dot(mc, wo_ref[ATTN_W:ATTN_W + CONV_W, :])
    x1 = x + ga1 * (_rms(mix) * g_post)
    h = (_rms(x1) * g_pre) * (1.0 + sc2) + sh2
    hb = h.astype(BF16)
    hl = (h - hb.astype(F32)).astype(BF16)
    logits = _dot(hb, wr_hi) + _dot(hb, wr_lo) + _dot(hl, wr_hi) + br
    rows = x.shape[0]
    col = lax.broadcasted_iota(jnp.int32, (rows, N_EXPERTS), 1)
    work = logits
    vals, idxs = [], []
    for _ in range(TOP_K):
        mx = jnp.max(work, axis=1, keepdims=True)
        sel = jnp.min(jnp.where(work == mx, col, N_EXPERTS), axis=1, keepdims=True)
        vals.append(mx)
        idxs.append(sel)
        work = jnp.where(col == sel, -jnp.inf, work)
    es = [jnp.exp(v - vals[0]) for v in vals]
    den = es[0] + es[1] + es[2] + es[3]
    lane = lax.broadcasted_iota(jnp.int32, (rows, LANES), 1)
    idx_out = jnp.zeros((rows, LANES), jnp.int32)
    w_out = jnp.zeros((rows, LANES), F32)
    for kk in range(TOP_K):
        idx_out = jnp.where(lane == kk, idxs[kk], idx_out)
        w_out = jnp.where(lane == kk, es[kk] / den, w_out)
    return x1, hb, idx_out, w_out


def _postmix_prompt_kernel(x_ref, o_ref, mc_ref, wo_ref, ga1_ref, sh2_ref, sc2_ref, gpost_ref, gpre_ref,
                           wrh_ref, wrl_ref, br_ref, x1_ref, h_ref, idx_ref, w_ref):
    x1, hb, idx, w = _postmix_body(x_ref[0], o_ref[0], mc_ref[0], wo_ref, ga1_ref[0], gpost_ref[...],
                                   gpre_ref[...], sh2_ref[0], sc2_ref[0], wrh_ref[...], wrl_ref[...],
                                   br_ref[...])
    x1_ref[0] = x1
    h_ref[0] = hb
    idx_ref[0] = idx
    w_ref[0] = w


def _postmix_prompt(x, o, mc, wo_b, ga1, sh2, sc2, g_post, g_pre, wr_hi, wr_lo, br):
    b, s, _ = x.shape
    tile = lambda w: pl.BlockSpec((1, TM, w), lambda i, j: (i, j, 0))
    per_b = pl.BlockSpec((1, 1, D_MODEL), lambda i, j: (i, 0, 0))
    const = lambda shape: pl.BlockSpec(shape, lambda i, j: tuple(0 for _ in shape))
    r3 = lambda a: a.reshape(b, 1, -1)
    r2 = lambda a: a.reshape(1, -1)
    return pl.pallas_call(
        _postmix_prompt_kernel,
        grid=(b, s // TM),
        in_specs=[tile(D_MODEL), tile(ATTN_W), tile(CONV_W), const(wo_b.shape), per_b, per_b, per_b,
                  const((1, D_MODEL)), const((1, D_MODEL)), const(wr_hi.shape), const(wr_lo.shape),
                  const((1, N_EXPERTS))],
        out_specs=[tile(D_MODEL), tile(D_MODEL), tile(LANES), tile(LANES)],
        out_shape=[jax.ShapeDtypeStruct((b, s, D_MODEL), F32), jax.ShapeDtypeStruct((b, s, D_MODEL), BF16),
                   jax.ShapeDtypeStruct((b, s, LANES), jnp.int32), jax.ShapeDtypeStruct((b, s, LANES), F32)],
        compiler_params=_cparams(("arbitrary", "arbitrary")),
        name="postmix_prompt",
    )(x, o, mc, wo_b, r3(ga1), r3(sh2), r3(sc2), r2(g_post), r2(g_pre), wr_hi, wr_lo, r2(br))


def _postmix_sample_kernel(x_ref, o_ref, mc_ref, wo_ref, ga1_ref, sh2_ref, sc2_ref, gpost_ref, gpre_ref,
                           wrh_ref, wrl_ref, br_ref, x1_ref, h_ref, idx_ref, w_ref):
    x1, hb, idx, w = _postmix_body(x_ref[...], o_ref[...].astype(BF16), mc_ref[...], wo_ref, ga1_ref[...],
                                   gpost_ref[...], gpre_ref[...], sh2_ref[...], sc2_ref[...], wrh_ref[...],
                                   wrl_ref[...], br_ref[...])
    x1_ref[...] = x1
    h_ref[...] = hb
    idx_ref[...] = idx
    w_ref[...] = w


def _postmix_sample(x, o, mc, wo_b, ga1, sh2, sc2, g_post, g_pre, wr_hi, wr_lo, br):
    n = x.shape[0]
    r2 = lambda a: a.reshape(1, -1)
    args = (x, o, mc, wo_b, ga1, sh2, sc2, r2(g_post), r2(g_pre), wr_hi, wr_lo, r2(br))
    outs = [jax.ShapeDtypeStruct((n, D_MODEL), F32), jax.ShapeDtypeStruct((n, D_MODEL), BF16),
            jax.ShapeDtypeStruct((n, LANES), jnp.int32), jax.ShapeDtypeStruct((n, LANES), F32)]
    return pl.pallas_call(
        _postmix_sample_kernel,
        grid=(1,),
        in_specs=[pl.BlockSpec(a.shape, lambda i: (0, 0)) for a in args],
        out_specs=[pl.BlockSpec(o_.shape, lambda i: (0, 0)) for o_ in outs],
        out_shape=outs,
        compiler_params=_cparams(("arbitrary",)),
        name="postmix_sample",
    )(*args)


def _moe_kernel(te_ref, nt_ref, x_ref, wgu_ref, bgu_ref, wd_ref, bd_ref, y_ref, wgu_b, wd_b):
    i = pl.program_id(0)
    e = te_ref[i]
    prev = te_ref[jnp.maximum(i - 1, 0)]
    used = i < nt_ref[0]

    @pl.when(jnp.logical_and(used, jnp.logical_or(i == 0, e != prev)))
    def _():
        wgu_b[...] = wgu_ref[0].astype(BF16)
        wd_b[...] = wd_ref[0].astype(BF16)

    @pl.when(used)
    def _():
        gu = _dot(x_ref[...], wgu_b[...]) + bgu_ref[0]
        g = jnp.minimum(gu[:, :D_FF], SWIGLU_LIMIT)
        u = jnp.clip(gu[:, D_FF:], -SWIGLU_LIMIT, SWIGLU_LIMIT)
        act = (u + 1.0) * g * jax.nn.sigmoid(SWIGLU_ALPHA * g)
        y_ref[...] = _dot(act.astype(BF16), wd_b[...]) + bd_ref[0]

    @pl.when(jnp.logical_not(used))
    def _():
        y_ref[...] = jnp.zeros_like(y_ref)


def _moe(tile_expert, n_tiles_used, h_sorted, w_gate_up, b_gate_up, w_down, b_down):
    p = h_sorted.shape[0]
    grid_spec = pltpu.PrefetchScalarGridSpec(
        num_scalar_prefetch=2,
        grid=(p // TM_E,),
        in_specs=[pl.BlockSpec((TM_E, D_MODEL), lambda i, te, nt: (i, 0)),
                  pl.BlockSpec((1, D_MODEL, 2 * D_FF), lambda i, te, nt: (te[i], 0, 0)),
                  pl.BlockSpec((1, 1, 2 * D_FF), lambda i, te, nt: (te[i], 0, 0)),
                  pl.BlockSpec((1, D_FF, D_MODEL), lambda i, te, nt: (te[i], 0, 0)),
                  pl.BlockSpec((1, 1, D_MODEL), lambda i, te, nt: (te[i], 0, 0))],
        out_specs=pl.BlockSpec((TM_E, D_MODEL), lambda i, te, nt: (i, 0)),
        scratch_shapes=[pltpu.VMEM((D_MODEL, 2 * D_FF), BF16), pltpu.VMEM((D_FF, D_MODEL), BF16)],
    )
    return pl.pallas_call(
        _moe_kernel,
        grid_spec=grid_spec,
        out_shape=jax.ShapeDtypeStruct((p, D_MODEL), F32),
        compiler_params=_cparams(("arbitrary",)),
        name="moe_experts",
    )(tile_expert, n_tiles_used, h_sorted, w_gate_up, b_gate_up.reshape(N_EXPERTS, 1, -1),
      w_down, b_down.reshape(N_EXPERTS, 1, -1))


def _final_body(x1, yg_ref, w, ga2, g_post):
    ffn = yg_ref[0] * w[:, 0:1]
    for kk in range(1, TOP_K):
        ffn = ffn + yg_ref[kk] * w[:, kk:kk + 1]
    return x1 + ga2 * (_rms(ffn) * g_post)


def _final_prompt_kernel(x1_ref, yg_ref, w_ref, ga2_ref, g_ref, o_ref):
    o_ref[...] = _final_body(x1_ref[...], yg_ref, w_ref[...], ga2_ref[0], g_ref[...])


def _final_prompt(x1, yg, w, ga2, g_post, seq):
    t = x1.shape[0]
    per_b = seq // TM
    return pl.pallas_call(
        _final_prompt_kernel,
        grid=(t // TM,),
        in_specs=[pl.BlockSpec((TM, D_MODEL), lambda i: (i, 0)),
                  pl.BlockSpec((TOP_K, TM, D_MODEL), lambda i: (0, i, 0)),
                  pl.BlockSpec((TM, LANES), lambda i: (i, 0)),
                  pl.BlockSpec((1, 1, D_MODEL), lambda i: (i // per_b, 0, 0)),
                  pl.BlockSpec((1, D_MODEL), lambda i: (0, 0))],
        out_specs=pl.BlockSpec((TM, D_MODEL), lambda i: (i, 0)),
        out_shape=jax.ShapeDtypeStruct((t, D_MODEL), F32),
        compiler_params=_cparams(("arbitrary",)),
        name="final_prompt",
    )(x1, yg, w, ga2.reshape(ga2.shape[0], 1, -1), g_post.reshape(1, -1))


def _final_sample_kernel(x1_ref, yg_ref, w_ref, ga2_ref, g_ref, o_ref):
    o_ref[...] = _final_body(x1_ref[...], yg_ref, w_ref[...], ga2_ref[...], g_ref[...])


def _final_sample(x1, yg, w, ga2, g_post):
    args = (x1, yg, w, ga2, g_post.reshape(1, -1))
    return pl.pallas_call(
        _final_sample_kernel,
        grid=(1,),
        in_specs=[pl.BlockSpec(a.shape, lambda i, nd=a.ndim: (0,) * nd) for a in args],
        out_specs=pl.BlockSpec(x1.shape, lambda i: (0, 0)),
        out_shape=jax.ShapeDtypeStruct(x1.shape, F32),
        compiler_params=_cparams(("arbitrary",)),
        name="final_sample",
    )(*args)


def _rope_tables(pos):
    half = HEAD_DIM // 2
    inv = ROPE_THETA ** (-jnp.arange(half, dtype=F32) / half)
    ang = pos.astype(F32)[:, None] * inv[None, :]
    cos, sin = jnp.cos(ang), jnp.sin(ang)
    cos128 = jnp.tile(cos, (1, LANES // half))
    sin128 = jnp.tile(jnp.concatenate([-sin, sin], axis=1), (1, LANES // HEAD_DIM))
    return cos128, sin128


def _route_plan(top_idx):
    n = top_idx.shape[0] * TOP_K
    e_flat = top_idx.reshape(-1)
    order = jnp.argsort(e_flat, stable=True)
    e_sorted = e_flat[order]
    counts = jnp.zeros((N_EXPERTS,), jnp.int32).at[e_flat].add(1)
    starts = jnp.cumsum(counts) - counts
    tiles = (counts + TM_E - 1) // TM_E
    tile_starts = jnp.cumsum(tiles) - tiles
    slot_sorted = tile_starts[e_sorted] * TM_E + (jnp.arange(n, dtype=jnp.int32) - starts[e_sorted])
    n_slots = (n // TM_E + N_EXPERTS) * TM_E
    slot_of_pair = jnp.zeros((n,), jnp.int32).at[order].set(slot_sorted)
    token_of_slot = jnp.zeros((n_slots,), jnp.int32).at[slot_sorted].set(order // TOP_K)
    n_used = jnp.sum(tiles)
    tile_id = jnp.arange(n_slots // TM_E, dtype=jnp.int32)
    tile_expert = jnp.searchsorted(jnp.cumsum(tiles), jnp.minimum(tile_id, n_used - 1), side="right")
    return slot_of_pair, token_of_slot, tile_expert.astype(jnp.int32), n_used.reshape(1).astype(jnp.int32)


def kernel(x_prompt, x_sample, c_prompt, c_sample, cache_k, cache_v, state_conv, page_table, w_ada, b_ada,
           g_pre_mix, g_post_mix, g_pre_ffn, g_post_ffn, w_in, lam_q1, lam_k1, lam_q2, lam_k2, g_subln,
           conv_w, w_out, w_router, b_router, w_gate_up, b_gate_up, w_down, b_down):
    nb, seq, _ = x_prompt.shape
    ns = x_sample.shape[0]
    n_phys, page_size = cache_k.shape[1], cache_k.shape[2]
    past_len = page_table.shape[1] * page_size
    layer = 0

    w_in_b = w_in[layer].astype(BF16)
    w_out_b = w_out[layer].astype(BF16)
    wr = w_router[layer]
    wr_hi = wr.astype(BF16)
    wr_lo = (wr - wr_hi.astype(F32)).astype(BF16)
    lam_vecs = jnp.stack([lam_q1[layer], lam_k1[layer], lam_q2[layer], lam_k2[layer]])

    mod = _ada(jnp.concatenate([c_prompt, c_sample], axis=0), w_ada[layer], b_ada[layer])
    sh1, sc1, ga1, sh2, sc2, ga2 = jnp.split(mod, 6, axis=-1)

    cos_p, sin_p = _rope_tables(jnp.arange(seq, dtype=jnp.int32))
    qb, k_p, v_p, kb, vb, mc_p, conv_p = _premix_prompt(
        x_prompt, sh1[:nb], sc1[:nb], g_pre_mix[layer], w_in_b, cos_p, sin_p, conv_w[layer])
    o_p = _attn_prompt(lam_vecs, g_subln[layer], qb, kb, vb)
    x1_p, h_p, idx_p, wgt_p = _postmix_prompt(
        x_prompt, o_p, mc_p, w_out_b, ga1[:nb], sh2[:nb], sc2[:nb], g_post_mix[layer], g_pre_ffn[layer],
        wr_hi, wr_lo, b_router[layer])

    cos_s, sin_s = _rope_tables(past_len + jnp.arange(1, dtype=jnp.int32))
    xs = x_sample.reshape(ns, D_MODEL)
    prefix = state_conv[layer]
    q_s, k_s, v_s, mc_s, u_s = _premix_sample(
        xs, sh1[nb:], sc1[nb:], g_pre_mix[layer], w_in_b, cos_s, sin_s, conv_w[layer],
        prefix[:, 0], prefix[:, 1])
    rows8 = lambda a: jnp.tile(a.reshape(ns, N_HEADS, V_DIM), (1, SUBLANES // N_HEADS, 1))
    ck2 = cache_k[layer].reshape(n_phys * page_size * N_HEADS, V_DIM)
    cv2 = cache_v[layer].reshape(n_phys * page_size * N_HEADS, V_DIM)
    o_s = _attn_decode(page_table, lam_vecs, g_subln[layer], rows8(q_s), rows8(k_s), rows8(v_s), ck2, cv2,
                       page_size * N_HEADS)
    x1_s, h_s, idx_s, wgt_s = _postmix_sample(
        xs, o_s.reshape(ns, ATTN_W), mc_s, w_out_b, ga1[nb:], sh2[nb:], sc2[nb:], g_post_mix[layer],
        g_pre_ffn[layer], wr_hi, wr_lo, b_router[layer])

    t_p = nb * seq
    h_all = jnp.concatenate([h_p.reshape(t_p, D_MODEL), h_s], axis=0)
    idx_all = jnp.concatenate([idx_p.reshape(t_p, LANES), idx_s], axis=0)[:, :TOP_K]
    slot_of_pair, token_of_slot, tile_expert, n_used = _route_plan(idx_all)
    y_sorted = _moe(tile_expert, n_used, h_all[token_of_slot], w_gate_up[layer], b_gate_up[layer],
                    w_down[layer], b_down[layer])
    yg = y_sorted[slot_of_pair.reshape(-1, TOP_K).T]

    y_p = _final_prompt(x1_p.reshape(t_p, D_MODEL), yg[:, :t_p], wgt_p.reshape(t_p, LANES), ga2[:nb],
                        g_post_ffn[layer], seq)
    y_s = _final_sample(x1_s, yg[:, t_p:], wgt_s, ga2[nb:], g_post_ffn[layer])

    return (y_p.reshape(nb, seq, D_MODEL),
            y_s.reshape(ns, 1, D_MODEL),
            k_p.reshape(1, nb, seq, N_HEADS, V_DIM),
            v_p.reshape(1, nb, seq, N_HEADS, V_DIM),
            conv_p.reshape(1, nb, CONV_K - 1, CONV_W),
            k_s.reshape(1, ns, 1, N_HEADS, V_DIM),
            v_s.reshape(1, ns, 1, N_HEADS, V_DIM),
            jnp.stack([prefix[:, 1], u_s], axis=1).reshape(1, ns, CONV_K - 1, CONV_W))
```
